```python
import jax, jax.numpy as jnp
from jax import lax
import numpy as np

D_MODEL = 2048
BATCH = 4
SEQ = 4096
DEPTH = 4

HEAD_DIM = 128
N_Q_HEADS = D_MODEL // HEAD_DIM
N_KV_HEADS = max(N_Q_HEADS // 4, 1)
GQA_GROUP = N_Q_HEADS // N_KV_HEADS
WINDOW = 128
BLOCK = 128
ROPE_DIM = HEAD_DIM // 4
ROPE_THETA = 500000.0
D_CONV = D_MODEL
CONV_WIDTH = 31
D_FF = ((8 * D_MODEL // 3 + 255) // 256) * 256
D_Q = N_Q_HEADS * HEAD_DIM
D_KV = N_KV_HEADS * HEAD_DIM
D_IN = D_Q + 2 * D_KV + 2 * D_CONV + 2 * D_MODEL
SPLITS = (D_Q, D_Q + D_KV, D_Q + 2 * D_KV, D_Q + 2 * D_KV + D_CONV,
          D_Q + 2 * D_KV + 2 * D_CONV, D_Q + 2 * D_KV + 2 * D_CONV + D_MODEL)
N_MOD = 6
DEEPNORM_ALPHA = (2.0 * DEPTH) ** 0.25
DEEPNORM_BETA = (8.0 * DEPTH) ** -0.25
LN_EPS = 1e-5
NEG_INF = -1e30

kernel_name = 'hybrid_swa_conformer_deepnorm_adaln'


def layer_norm(x, g, b):
    xf = x.astype(jnp.float32)
    mu = jnp.mean(xf, axis=-1, keepdims=True)
    xc = xf - mu
    var = jnp.mean(xc * xc, axis=-1, keepdims=True)
    y = xc * lax.rsqrt(var + LN_EPS) * g.astype(jnp.float32) + b.astype(jnp.float32)
    return y.astype(x.dtype)


def partial_rotary(t, cos, sin):
    half = ROPE_DIM // 2
    tf = t[..., :ROPE_DIM].astype(jnp.float32)
    t1, t2 = tf[..., :half], tf[..., half:]
    cs, sn = cos[None, :, None, :], sin[None, :, None, :]
    rot = jnp.concatenate([t1 * cs - t2 * sn, t2 * cs + t1 * sn], axis=-1).astype(t.dtype)
    return jnp.concatenate([rot, t[..., ROPE_DIM:]], axis=-1)


def _band(t, nb):
    B, _, H, D = t.shape
    tp = jnp.pad(t, ((0, 0), (BLOCK, BLOCK), (0, 0), (0, 0))).reshape(B, nb + 2, BLOCK, H, D)
    return jnp.concatenate([tp[:, :-2], tp[:, 1:-1], tp[:, 2:]], axis=2)


def window_gqa_with_sink(q, k, v, sink):
    B, S = q.shape[0], q.shape[1]
    nb = S // BLOCK
    qb = q.reshape(B, nb, BLOCK, N_KV_HEADS, GQA_GROUP, HEAD_DIM)
    kb, vb = _band(k, nb), _band(v, nb)
    s = jnp.einsum('bnqhgd,bnkhd->bnhgqk', qb, kb).astype(jnp.float32) * (HEAD_DIM ** -0.5)
    q_pos = jnp.arange(nb)[:, None] * BLOCK + jnp.arange(BLOCK)[None, :]
    k_pos = jnp.arange(nb)[:, None] * BLOCK - BLOCK + jnp.arange(3 * BLOCK)[None, :]
    valid = ((k_pos >= 0) & (k_pos < S))[:, None, :] & \
        (jnp.abs(k_pos[:, None, :] - q_pos[:, :, None]) <= WINDOW)
    s = jnp.where(valid[None, :, None, None], s, NEG_INF)
    sink_l = sink.astype(jnp.float32).reshape(N_KV_HEADS, GQA_GROUP)[None, None, :, :, None, None]
    m = jnp.maximum(jnp.max(s, axis=-1, keepdims=True), sink_l)
    p = jnp.exp(s - m)
    denom = jnp.sum(p, axis=-1, keepdims=True) + jnp.exp(sink_l - m)
    o = jnp.einsum('bnhgqk,bnkhd->bnqhgd', (p / denom).astype(v.dtype), vb)
    return o.reshape(B, S, N_Q_HEADS * HEAD_DIM)


def conformer_conv(glu_a, glu_b, w_dw, ln_g, ln_b):
    u = glu_a * jax.nn.sigmoid(glu_b)
    u = lax.conv_general_dilated(
        u, w_dw[:, None, :].astype(u.dtype), window_strides=(1,),
        padding=[(CONV_WIDTH // 2, CONV_WIDTH // 2)],
        dimension_numbers=('NWC', 'WIO', 'NWC'), feature_group_count=u.shape[-1])
    return jax.nn.silu(layer_norm(u, ln_g, ln_b))


def setup_inputs(seed: int = 0) -> dict:
    key = jax.random.key(seed)
    ks = jax.random.split(key, 18)
    L = DEPTH

    def nrm(k, shape, scale):
        return jax.random.normal(k, shape, jnp.float32) * scale

    return {
        'x': nrm(ks[0], (BATCH, SEQ, D_MODEL), 1.0),
        'c': nrm(ks[1], (BATCH, D_MODEL), 1.0),
        'w_ada': nrm(ks[2], (L, D_MODEL, N_MOD * D_MODEL), 0.3 * D_MODEL ** -0.5),
        'b_ada': nrm(ks[3], (L, N_MOD * D_MODEL), 0.02),
        'w_in': nrm(ks[4], (L, D_MODEL, D_IN), D_MODEL ** -0.5),
        'sink': nrm(ks[5], (L, N_Q_HEADS), 1.0),
        'w_dw': nrm(ks[6], (L, CONV_WIDTH, D_CONV), CONV_WIDTH ** -0.5),
        'conv_ln_g': 1.0 + nrm(ks[7], (L, D_CONV), 0.02),
        'conv_ln_b': nrm(ks[8], (L, D_CONV), 0.02),
        'w_oa': nrm(ks[9], (L, D_Q, D_MODEL), D_Q ** -0.5),
        'w_ob': nrm(ks[10], (L, D_CONV, D_MODEL), D_CONV ** -0.5),
        'w_out': nrm(ks[11], (L, D_MODEL, D_MODEL), DEEPNORM_BETA * D_MODEL ** -0.5),
        'ln1_g': 1.0 + nrm(ks[12], (L, D_MODEL), 0.02),
        'ln1_b': nrm(ks[13], (L, D_MODEL), 0.02),
        'w_gu': nrm(ks[14], (L, D_MODEL, 2 * D_FF), D_MODEL ** -0.5),
        'w_down': nrm(ks[15], (L, D_FF, D_MODEL), DEEPNORM_BETA * D_FF ** -0.5),
        'ln2_g': 1.0 + nrm(ks[16], (L, D_MODEL), 0.02),
        'ln2_b': nrm(ks[17], (L, D_MODEL), 0.02),
    }


def reference(x, c, w_ada, b_ada, w_in, sink, w_dw, conv_ln_g, conv_ln_b, w_oa, w_ob, w_out,
              ln1_g, ln1_b, w_gu, w_down, ln2_g, ln2_b):
    B, S, _ = x.shape
    pos = jnp.arange(S, dtype=jnp.float32)
    inv_freq = ROPE_THETA ** (-jnp.arange(0, ROPE_DIM, 2, dtype=jnp.float32) / ROPE_DIM)
    ang = pos[:, None] * inv_freq[None, :]
    cos, sin = jnp.cos(ang), jnp.sin(ang)
    c_act = jax.nn.silu(c)
    for l in range(DEPTH):
        mod = (c_act @ w_ada[l] + b_ada[l])[:, None, :]
        sh_a, sc_a, gt_a, sh_f, sc_f, gt_f = jnp.split(mod, N_MOD, axis=-1)
        h = x * (1 + sc_a) + sh_a
        q, k, v, glu_a, glu_b, g_a, g_b = jnp.split(h @ w_in[l], SPLITS, axis=-1)
        q = partial_rotary(q.reshape(B, S, N_Q_HEADS, HEAD_DIM), cos, sin)
        k = partial_rotary(k.reshape(B, S, N_KV_HEADS, HEAD_DIM), cos, sin)
        v = v.reshape(B, S, N_KV_HEADS, HEAD_DIM)
        y_a = window_gqa_with_sink(q, k, v, sink[l]) @ w_oa[l]
        y_b = conformer_conv(glu_a, glu_b, w_dw[l], conv_ln_g[l], conv_ln_b[l]) @ w_ob[l]
        merged = jax.nn.sigmoid(g_a) * y_a + jax.nn.sigmoid(g_b) * y_b
        x = layer_norm(DEEPNORM_ALPHA * x + (1 + gt_a) * (merged @ w_out[l]), ln1_g[l], ln1_b[l])
        h = x * (1 + sc_f) + sh_f
        gate, up = jnp.split(h @ w_gu[l], 2, axis=-1)
        ffn = (jax.nn.silu(gate) * up) @ w_down[l]
        x = layer_norm(DEEPNORM_ALPHA * x + (1 + gt_f) * ffn, ln2_g[l], ln2_b[l])
    return x
```

```python
import functools

import jax
import jax.numpy as jnp
from jax import lax
from jax.experimental import pallas as pl
from jax.experimental.pallas import tpu as pltpu

F32 = jnp.float32
BF16 = jnp.bfloat16

D_MODEL = 2048
HEAD_DIM = 128
N_Q_HEADS = 16
N_KV_HEADS = 4
GQA_GROUP = 4
WINDOW = 128
ROPE_DIM = 32
ROPE_THETA = 500000.0
CONV_WIDTH = 31
CONV_HALO = 16
D_FF = 5632
D_KV = N_KV_HEADS * HEAD_DIM
D_IN = D_MODEL + 2 * D_KV + 4 * D_MODEL
N_MOD = 6
DEPTH = 4
DEEPNORM_ALPHA = (2.0 * DEPTH) ** 0.25
LN_EPS = 1e-5
NEG_INF = -1e30

COL_Q, COL_K, COL_V = 0, D_MODEL, D_MODEL + D_KV
COL_GLU_A = D_MODEL + 2 * D_KV
COL_GLU_B = COL_GLU_A + D_MODEL
COL_GATE_A = COL_GLU_B + D_MODEL
COL_GATE_B = COL_GATE_A + D_MODEL
D_QKV = COL_GLU_A

VMEM_LIMIT_BYTES = 56 * 1024 * 1024

MOD_SH_A, MOD_SC_A, MOD_GT_A, MOD_SH_F, MOD_SC_F, MOD_GT_F = range(N_MOD)


def _params(n_axes):
    return pltpu.CompilerParams(dimension_semantics=("arbitrary",) * n_axes,
                                vmem_limit_bytes=VMEM_LIMIT_BYTES)


def _layer_norm(z, g, b):
    mu = jnp.mean(z, axis=-1, keepdims=True)
    zc = z - mu
    var = jnp.mean(zc * zc, axis=-1, keepdims=True)
    return zc * lax.rsqrt(var + LN_EPS) * g + b


def _mod_kernel(c_ref, w_ref, b_ref, o_ref):
    c = c_ref[...]
    c_act = (c * jax.nn.sigmoid(c)).astype(BF16)
    o_ref[...] = jnp.dot(c_act, w_ref[...].astype(BF16), preferred_element_type=F32) + b_ref[...]


def _modulation(c, w_ada, b_ada):
    depth, _, n_out = w_ada.shape
    batch = c.shape[0]
    rows = 8
    c_pad = jnp.zeros((rows, D_MODEL), F32).at[:batch].set(c)
    tn = 1024
    out = pl.pallas_call(
        _mod_kernel,
        grid=(depth, n_out // tn),
        in_specs=[
            pl.BlockSpec((rows, D_MODEL), lambda l, j: (0, 0)),
            pl.BlockSpec((None, D_MODEL, tn), lambda l, j: (l, 0, j)),
            pl.BlockSpec((None, 1, tn), lambda l, j: (l, 0, j)),
        ],
        out_specs=pl.BlockSpec((None, rows, tn), lambda l, j: (l, 0, j)),
        out_shape=jax.ShapeDtypeStruct((depth, rows, n_out), F32),
        compiler_params=_params(2),
        name="adaln_modulation",
    )(c_pad, w_ada, b_ada.reshape(depth, 1, n_out))
    return out[:, :batch].reshape(depth, batch, N_MOD, 1, D_MODEL)


def _mod_spec(layer, which, rows_per_batch_tile, m_axis):
    def index_map(*ids):
        return (layer, ids[m_axis] // rows_per_batch_tile, which, 0, 0)
    return pl.BlockSpec((None, None, None, 1, D_MODEL), index_map)


def _modulate_kernel(x_ref, sc_ref, sh_ref, h_ref):
    h_ref[...] = (x_ref[...] * (1.0 + sc_ref[...]) + sh_ref[...]).astype(BF16)


def _modulate(x2, mod, layer, seq):
    m_rows = x2.shape[0]
    tm = 1024
    return pl.pallas_call(
        _modulate_kernel,
        grid=(m_rows // tm,),
        in_specs=[
            pl.BlockSpec((tm, D_MODEL), lambda m: (m, 0)),
            _mod_spec(layer, MOD_SC_A, seq // tm, 0),
            _mod_spec(layer, MOD_SH_A, seq // tm, 0),
        ],
        out_specs=pl.BlockSpec((tm, D_MODEL), lambda m: (m, 0)),
        out_shape=jax.ShapeDtypeStruct((m_rows, D_MODEL), BF16),
        compiler_params=_params(1),
        name="modulate_in",
    )(x2, mod, mod)


def _rotary(t, cos_t, sin_lo, sin_hi):
    half = ROPE_DIM // 2
    return (t * cos_t + pltpu.roll(t, HEAD_DIM - half, 1) * sin_lo + pltpu.roll(t, half, 1) * sin_hi)


def _qkv_kernel(h_ref, w_ref, cos_ref, slo_ref, shi_ref, o_ref, wbf_ref, *, n_rot_tiles, heads_per_tile):
    n = pl.program_id(0)
    m = pl.program_id(1)

    @pl.when(m == 0)
    def _():
        wbf_ref[...] = w_ref[...].astype(BF16)

    acc = jnp.dot(h_ref[...], wbf_ref[...], preferred_element_type=F32)
    cos_t, sin_lo, sin_hi = cos_ref[...], slo_ref[...], shi_ref[...]

    def store(rot_heads):
        for j in range(heads_per_tile):
            sl = slice(j * HEAD_DIM, (j + 1) * HEAD_DIM)
            t = acc[:, sl]
            if j < rot_heads:
                t = _rotary(t, cos_t, sin_lo, sin_hi)
            o_ref[:, sl] = t.astype(BF16)

    @pl.when(n < n_rot_tiles)
    def _():
        store(heads_per_tile)

    @pl.when(n == n_rot_tiles)
    def _():
        store(N_KV_HEADS)


def _qkv_proj(h, w_in, layer, rope, seq):
    m_rows = h.shape[0]
    tm, tn = 1024, 1024
    assert D_MODEL % tn == 0 and 2 * D_KV == tn
    kern = functools.partial(_qkv_kernel, n_rot_tiles=D_MODEL // tn, heads_per_tile=tn // HEAD_DIM)
    rope_spec = pl.BlockSpec((tm, HEAD_DIM), lambda n, m: (m % (seq // tm), 0))
    return pl.pallas_call(
        kern,
        grid=(D_QKV // tn, m_rows // tm),
        in_specs=[
            pl.BlockSpec((tm, D_MODEL), lambda n, m: (m, 0)),
            pl.BlockSpec((None, D_MODEL, tn), lambda n, m: (layer, 0, n)),
            rope_spec, rope_spec, rope_spec,
        ],
        out_specs=pl.BlockSpec((tm, tn), lambda n, m: (m, n)),
        out_shape=jax.ShapeDtypeStruct((m_rows, D_QKV), BF16),
        scratch_shapes=[pltpu.VMEM((D_MODEL, tn), BF16)],
        compiler_params=_params(2),
        name="qkv_proj",
    )(h, w_in, *rope)


def _glu_kernel(h_ref, wa_ref, wb_ref, o_ref, wabf_ref, wbbf_ref):
    @pl.when(pl.program_id(1) == 0)
    def _():
        wabf_ref[...] = wa_ref[...].astype(BF16)
        wbbf_ref[...] = wb_ref[...].astype(BF16)

    h = h_ref[...]
    a = jnp.dot(h, wabf_ref[...], preferred_element_type=F32)
    b = jnp.dot(h, wbbf_ref[...], preferred_element_type=F32)
    o_ref[...] = a * jax.nn.sigmoid(b)


def _glu_proj(h, w_in, layer):
    m_rows = h.shape[0]
    tm, tn = 1024, 512
    a0, b0 = COL_GLU_A // tn, COL_GLU_B // tn
    return pl.pallas_call(
        _glu_kernel,
        grid=(D_MODEL // tn, m_rows // tm),
        in_specs=[
            pl.BlockSpec((tm, D_MODEL), lambda n, m: (m, 0)),
            pl.BlockSpec((None, D_MODEL, tn), lambda n, m: (layer, 0, a0 + n)),
            pl.BlockSpec((None, D_MODEL, tn), lambda n, m: (layer, 0, b0 + n)),
        ],
        out_specs=pl.BlockSpec((tm, tn), lambda n, m: (m, n)),
        out_shape=jax.ShapeDtypeStruct((m_rows, D_MODEL), F32),
        scratch_shapes=[pltpu.VMEM((D_MODEL, tn), BF16)] * 2,
        compiler_params=_params(2),
        name="glu_proj",
    )(h, w_in, w_in)


def _attn_kernel(sink_ref, q_ref, kp_ref, km_ref, kn_ref, vp_ref, vm_ref, vn_ref, o_ref, *, tq, seq):
    i = pl.program_id(1)
    hkv = pl.program_id(2)
    k_all = jnp.concatenate([kp_ref[...], km_ref[...], kn_ref[...]], axis=0)
    v_all = jnp.concatenate([vp_ref[...], vm_ref[...], vn_ref[...]], axis=0)
    rows = GQA_GROUP * WINDOW
    band = 3 * WINDOW
    r = lax.broadcasted_iota(jnp.int32, (rows, band), 0) % WINDOW
    c = lax.broadcasted_iota(jnp.int32, (rows, band), 1)
    rel = c - r
    in_band = (rel >= 0) & (rel <= 2 * WINDOW)
    sink_col = jnp.concatenate(
        [jnp.full((WINDOW, 1), sink_ref[hkv * GQA_GROUP + g], F32) for g in range(GQA_GROUP)], axis=0)
    scale = HEAD_DIM ** -0.5
    for j in range(tq // WINDOW):
        k_pos = i * tq + (j - 1) * WINDOW + c
        valid = in_band & (k_pos >= 0) & (k_pos < seq)
        kb = k_all[j * WINDOW:j * WINDOW + band]
        vb = v_all[j * WINDOW:j * WINDOW + band]
        qs = jnp.concatenate(
            [q_ref[j * WINDOW:(j + 1) * WINDOW, g * HEAD_DIM:(g + 1) * HEAD_DIM] for g in range(GQA_GROUP)],
            axis=0)
        s = lax.dot_general(qs, kb, (((1,), (1,)), ((), ())), preferred_element_type=F32) * scale
        s = jnp.where(valid, s, NEG_INF)
        mx = jnp.maximum(jnp.max(s, axis=-1, keepdims=True), sink_col)
        p = jnp.exp(s - mx)
        denom = jnp.sum(p, axis=-1, keepdims=True) + jnp.exp(sink_col - mx)
        o = jnp.dot(p.astype(BF16), vb, preferred_element_type=F32) / denom
        for g in range(GQA_GROUP):
            o_ref[j * WINDOW:(j + 1) * WINDOW, g * HEAD_DIM:(g + 1) * HEAD_DIM] = (
                o[g * WINDOW:(g + 1) * WINDOW].astype(BF16))


def _attention(qkv, sink_l, batch, seq):
    tq = 512
    n_blk = seq // WINDOW
    per_tile = tq // WINDOW
    qkv3 = qkv.reshape(batch, seq, D_QKV)
    k_col, v_col = COL_K // HEAD_DIM, COL_V // HEAD_DIM
    q_cols = GQA_GROUP * HEAD_DIM

    def prev_map(col0):
        return lambda b, i, h: (b, jnp.maximum(i * per_tile - 1, 0), col0 + h)

    def main_map(col0):
        return lambda b, i, h: (b, i, col0 + h)

    def next_map(col0):
        return lambda b, i, h: (b, jnp.minimum((i + 1) * per_tile, n_blk - 1), col0 + h)

    edge = lambda fn: pl.BlockSpec((None, WINDOW, HEAD_DIM), fn)
    main = lambda fn: pl.BlockSpec((None, tq, HEAD_DIM), fn)
    kern = functools.partial(_attn_kernel, tq=tq, seq=seq)
    out = pl.pallas_call(
        kern,
        grid=(batch, seq // tq, N_KV_HEADS),
        in_specs=[
            pl.BlockSpec(memory_space=pltpu.SMEM),
            pl.BlockSpec((None, tq, q_cols), lambda b, i, h: (b, i, h)),
            edge(prev_map(k_col)), main(main_map(k_col)), edge(next_map(k_col)),
            edge(prev_map(v_col)), main(main_map(v_col)), edge(next_map(v_col)),
        ],
        out_specs=pl.BlockSpec((None, tq, q_cols), lambda b, i, h: (b, i, h)),
        out_shape=jax.ShapeDtypeStruct((batch, seq, D_MODEL), BF16),
        compiler_params=_params(3),
        name="window_attention",
    )(sink_l, qkv3, qkv3, qkv3, qkv3, qkv3, qkv3, qkv3)
    return out.reshape(batch * seq, D_MODEL)


def _conv_kernel(up_ref, um_ref, un_ref, w_ref, g_ref, b_ref, o_ref, xs_ref, acc_ref, *, ts, n_tiles):
    i = pl.program_id(1)
    xs_ref[0:CONV_HALO] = jnp.where(i > 0, up_ref[...], 0.0)
    xs_ref[CONV_HALO:CONV_HALO + ts] = um_ref[...]
    xs_ref[CONV_HALO + ts:] = jnp.where(i < n_tiles - 1, un_ref[...], 0.0)

    rb, cb = 32, 512
    first = CONV_HALO - CONV_WIDTH // 2
    win = rb + 2 * CONV_HALO

    def row_chunk(rc, carry):
        r0 = pl.multiple_of(rc * rb, rb)
        for cc in range(D_MODEL // cb):
            cs = slice(cc * cb, (cc + 1) * cb)
            window = xs_ref[pl.ds(r0, win), cs]
            acc = jnp.zeros((rb, cb), F32)
            for r in range(8):
                shifted = window if r == 0 else pltpu.roll(window, win - r, 0)
                for a in range(win // 8):
                    k = 8 * a + r - first
                    if 0 <= k < CONV_WIDTH:
                        acc = acc + shifted[8 * a:8 * a + rb] * w_ref[k:k + 1, cs]
            acc_ref[pl.ds(r0, rb), cs] = acc
        return carry

    lax.fori_loop(0, ts // rb, row_chunk, 0)
    y = _layer_norm(acc_ref[...], g_ref[...], b_ref[...])
    o_ref[...] = (y * jax.nn.sigmoid(y)).astype(BF16)


def _conformer_conv(u, w_dw, ln_g, ln_b, layer, batch, seq):
    ts = 256
    n_tiles = seq // ts
    per_tile = ts // CONV_HALO
    n_halo = seq // CONV_HALO
    u3 = u.reshape(batch, seq, D_MODEL)
    depth = w_dw.shape[0]
    kern = functools.partial(_conv_kernel, ts=ts, n_tiles=n_tiles)
    vec = pl.BlockSpec((None, 1, D_MODEL), lambda b, i: (layer, 0, 0))
    out = pl.pallas_call(
        kern,
        grid=(batch, n_tiles),
        in_specs=[
            pl.BlockSpec((None, CONV_HALO, D_MODEL), lambda b, i: (b, jnp.maximum(i * per_tile - 1, 0), 0)),
            pl.BlockSpec((None, ts, D_MODEL), lambda b, i: (b, i, 0)),
            pl.BlockSpec((None, CONV_HALO, D_MODEL),
                         lambda b, i: (b, jnp.minimum((i + 1) * per_tile, n_halo - 1), 0)),
            pl.BlockSpec((None, CONV_WIDTH, D_MODEL), lambda b, i: (layer, 0, 0)),
            vec, vec,
        ],
        out_specs=pl.BlockSpec((None, ts, D_MODEL), lambda b, i: (b, i, 0)),
        out_shape=jax.ShapeDtypeStruct((batch, seq, D_MODEL), BF16),
        scratch_shapes=[pltpu.VMEM((ts + 2 * CONV_HALO, D_MODEL), F32), pltpu.VMEM((ts, D_MODEL), F32)],
        compiler_params=_params(2),
        name="conformer_conv",
    )(u3, u3, u3, w_dw, ln_g.reshape(depth, 1, D_MODEL), ln_b.reshape(depth, 1, D_MODEL))
    return out.reshape(batch * seq, D_MODEL)


def _merge_kernel(h_ref, o_ref, cv_ref, wga_ref, wgb_ref, woa_ref, wob_ref, out_ref, wgabf_ref, wgbbf_ref):
    @pl.when(pl.program_id(1) == 0)
    def _():
        wgabf_ref[...] = wga_ref[...].astype(BF16)
        wgbbf_ref[...] = wgb_ref[...].astype(BF16)

    h = h_ref[...]
    g_a = jnp.dot(h, wgabf_ref[...], preferred_element_type=F32)
    g_b = jnp.dot(h, wgbbf_ref[...], preferred_element_type=F32)
    y_a = jnp.dot(o_ref[...], woa_ref[...], preferred_element_type=F32)
    y_b = jnp.dot(cv_ref[...], wob_ref[...], preferred_element_type=F32)
    out_ref[...] = (jax.nn.sigmoid(g_a) * y_a + jax.nn.sigmoid(g_b) * y_b).astype(BF16)


def _merge(h, o, cv, w_in, w_oa_bf, w_ob_bf, layer):
    m_rows = h.shape[0]
    tm, tn = 512, 512
    ga0, gb0 = COL_GATE_A // tn, COL_GATE_B // tn
    act = pl.BlockSpec((tm, D_MODEL), lambda n, m: (m, 0))
    wsq = pl.BlockSpec((None, D_MODEL, tn), lambda n, m: (layer, 0, n))
    return pl.pallas_call(
        _merge_kernel,
        grid=(D_MODEL // tn, m_rows // tm),
        in_specs=[
            act, act, act,
            pl.BlockSpec((None, D_MODEL, tn), lambda n, m: (layer, 0, ga0 + n)),
            pl.BlockSpec((None, D_MODEL, tn), lambda n, m: (layer, 0, gb0 + n)),
            wsq, wsq,
        ],
        out_specs=pl.BlockSpec((tm, tn), lambda n, m: (m, n)),
        out_shape=jax.ShapeDtypeStruct((m_rows, D_MODEL), BF16),
        scratch_shapes=[pltpu.VMEM((D_MODEL, tn), BF16)] * 2,
        compiler_params=_params(2),
        name="branch_merge",
    )(h, o, cv, w_in, w_in, w_oa_bf, w_ob_bf)


def _proj_norm_kernel(a_ref, w_ref, x_ref, gt_ref, g_ref, b_ref, *rest, emit_h):
    if emit_h:
        sc_ref, sh_ref, xo_ref, ho_ref = rest
    else:
        (xo_ref,) = rest
    y = jnp.dot(a_ref[...], w_ref[...], preferred_element_type=F32)
    z = DEEPNORM_ALPHA * x_ref[...] + (1.0 + gt_ref[...]) * y
    xn = _layer_norm(z, g_ref[...], b_ref[...])
    xo_ref[...] = xn
    if emit_h:
        ho_ref[...] = (xn * (1.0 + sc_ref[...]) + sh_ref[...]).astype(BF16)


def _proj_norm(a, w_bf, x2, mod, ln_g, ln_b, layer, gate_idx, next_mod, seq, tm, name):
    m_rows, k_dim = a.shape
    depth = ln_g.shape[0]
    per_batch = seq // tm
    vec = pl.BlockSpec((None, 1, D_MODEL), lambda m: (layer, 0, 0))
    in_specs = [
        pl.BlockSpec((tm, k_dim), lambda m: (m, 0)),
        pl.BlockSpec((None, k_dim, D_MODEL), lambda m: (layer, 0, 0), pipeline_mode=pl.Buffered(1)),
        pl.BlockSpec((tm, D_MODEL), lambda m: (m, 0)),
        _mod_spec(layer, gate_idx, per_batch, 0),
        vec, vec,
    ]
    args = [a, w_bf, x2, mod, ln_g.reshape(depth, 1, D_MODEL), ln_b.reshape(depth, 1, D_MODEL)]
    row_out = pl.BlockSpec((tm, D_MODEL), lambda m: (m, 0))
    emit_h = next_mod is not None
    if emit_h:
        nl, sc_idx, sh_idx = next_mod
        in_specs += [_mod_spec(nl, sc_idx, per_batch, 0), _mod_spec(nl, sh_idx, per_batch, 0)]
        args += [mod, mod]
        out_specs = [row_out, row_out]
        out_shape = [jax.ShapeDtypeStruct((m_rows, D_MODEL), F32), jax.ShapeDtypeStruct((m_rows, D_MODEL), BF16)]
    else:
        out_specs = [row_out]
        out_shape = [jax.ShapeDtypeStruct((m_rows, D_MODEL), F32)]
    outs = pl.pallas_call(
        functools.partial(_proj_norm_kernel, emit_h=emit_h),
        grid=(m_rows // tm,),
        in_specs=in_specs,
        out_specs=out_specs,
        out_shape=out_shape,
        compiler_params=_params(1),
        name=name,
    )(*args)
    return (outs[0], outs[1]) if emit_h else (outs[0], None)


def _swiglu_kernel(h_ref, wg_ref, wu_ref, o_ref, wgbf_ref, wubf_ref):
    @pl.when(pl.program_id(1) == 0)
    def _():
        wgbf_ref[...] = wg_ref[...].astype(BF16)
        wubf_ref[...] = wu_ref[...].astype(BF16)

    h = h_ref[...]
    gate = jnp.dot(h, wgbf_ref[...], preferred_element_type=F32)
    up = jnp.dot(h, wubf_ref[...], preferred_element_type=F32)
    o_ref[...] = (gate * jax.nn.sigmoid(gate) * up).astype(BF16)


def _swiglu_up(h, w_gu, layer):
    m_rows = h.shape[0]
    tm, tn = 1024, 512
    up0 = D_FF // tn
    return pl.pallas_call(
        _swiglu_kernel,
        grid=(D_FF // tn, m_rows // tm),
        in_specs=[
            pl.BlockSpec((tm, D_MODEL), lambda n, m: (m, 0)),
            pl.BlockSpec((None, D_MODEL, tn), lambda n, m: (layer, 0, n)),
            pl.BlockSpec((None, D_MODEL, tn), lambda n, m: (layer, 0, up0 + n)),
        ],
        out_specs=pl.BlockSpec((tm, tn), lambda n, m: (m, n)),
        out_shape=jax.ShapeDtypeStruct((m_rows, D_FF), BF16),
        scratch_shapes=[pltpu.VMEM((D_MODEL, tn), BF16)] * 2,
        compiler_params=_params(2),
        name="swiglu_up",
    )(h, w_gu, w_gu)


def _rope_tables(seq):
    half = ROPE_DIM // 2
    pos = jnp.arange(seq, dtype=F32)
    inv_freq = ROPE_THETA ** (-jnp.arange(0, ROPE_DIM, 2, dtype=F32) / ROPE_DIM)
    ang = pos[:, None] * inv_freq[None, :]
    cos, sin = jnp.cos(ang), jnp.sin(ang)
    zeros = lambda n: jnp.zeros((seq, n), F32)
    cos_t = jnp.concatenate([cos, cos, jnp.ones((seq, HEAD_DIM - ROPE_DIM), F32)], axis=1)
    sin_lo = jnp.concatenate([-sin, zeros(HEAD_DIM - half)], axis=1)
    sin_hi = jnp.concatenate([zeros(half), sin, zeros(HEAD_DIM - ROPE_DIM)], axis=1)
    return cos_t, sin_lo, sin_hi


def kernel(x, c, w_ada, b_ada, w_in, sink, w_dw, conv_ln_g, conv_ln_b, w_oa, w_ob, w_out,
           ln1_g, ln1_b, w_gu, w_down, ln2_g, ln2_b):
    batch, seq, _ = x.shape
    depth = w_in.shape[0]
    mod = _modulation(c, w_ada, b_ada)
    rope = _rope_tables(seq)
    w_oa_bf, w_ob_bf = w_oa.astype(BF16), w_ob.astype(BF16)
    w_out_bf, w_down_bf = w_out.astype(BF16), w_down.astype(BF16)

    x2 = x.reshape(batch * seq, D_MODEL)
    h = _modulate(x2, mod, 0, seq)
    for l in range(depth):
        qkv = _qkv_proj(h, w_in, l, rope, seq)
        u = _glu_proj(h, w_in, l)
        attn = _attention(qkv, sink[l], batch, seq)
        cv = _conformer_conv(u, w_dw, conv_ln_g, conv_ln_b, l, batch, seq)
        merged = _merge(h, attn, cv, w_in, w_oa_bf, w_ob_bf, l)
        x2, h = _proj_norm(merged, w_out_bf, x2, mod, ln1_g, ln1_b, l, MOD_GT_A,
                           (l, MOD_SC_F, MOD_SH_F), seq, 512, "attn_out_norm")
        act = _swiglu_up(h, w_gu, l)
        next_mod = (l + 1, MOD_SC_A, MOD_SH_A) if l + 1 < depth else None
        x2, h = _proj_norm(act, w_down_bf, x2, mod, ln2_g, ln2_b, l, MOD_GT_F,
                           next_mod, seq, 256, "ffn_down_norm")
    return x2.reshape(batch, seq, D_MODEL)
```

```python
import functools

import jax
import jax.numpy as jnp
from jax import lax
from jax.experimental import pallas as pl
from jax.experimental.pallas import tpu as pltpu

F32 = jnp.float32
BF16 = jnp.bfloat16

D_MODEL = 2048
HEAD_DIM = 128
N_Q_HEADS = 16
N_KV_HEADS = 4
GQA_GROUP = 4
WINDOW = 128
ROPE_DIM = 32
ROPE_THETA = 500000.0
CONV_WIDTH = 31
CONV_HALO = 16
D_FF = 5632
D_KV = N_KV_HEADS * HEAD_DIM
D_IN = D_MODEL + 2 * D_KV + 4 * D_MODEL
N_MOD = 6
DEPTH = 4
DEEPNORM_ALPHA = (2.0 * DEPTH) ** 0.25
LN_EPS = 1e-5
NEG_INF = -1e30

COL_Q, COL_K, COL_V = 0, D_MODEL, D_MODEL + D_KV
COL_GLU_A = D_MODEL + 2 * D_KV
COL_GLU_B = COL_GLU_A + D_MODEL
COL_GATE_A = COL_GLU_B + D_MODEL
COL_GATE_B = COL_GATE_A + D_MODEL

VMEM_LIMIT_BYTES = 56 * 1024 * 1024

MOD_SH_A, MOD_SC_A, MOD_GT_A, MOD_SH_F, MOD_SC_F, MOD_GT_F = range(N_MOD)


def _params(n_axes):
    return pltpu.CompilerParams(dimension_semantics=("arbitrary",) * n_axes,
                                vmem_limit_bytes=VMEM_LIMIT_BYTES)


def _layer_norm(z, g, b):
    mu = jnp.mean(z, axis=-1, keepdims=True)
    zc = z - mu
    var = jnp.mean(zc * zc, axis=-1, keepdims=True)
    return zc * lax.rsqrt(var + LN_EPS) * g + b


def _mod_kernel(c_ref, w_ref, b_ref, o_ref):
    c = c_ref[...]
    c_act = (c * jax.nn.sigmoid(c)).astype(BF16)
    o_ref[...] = jnp.dot(c_act, w_ref[...].astype(BF16), preferred_element_type=F32) + b_ref[...]


def _modulation(c, w_ada, b_ada):
    depth, _, n_out = w_ada.shape
    batch = c.shape[0]
    rows = 8
    c_pad = jnp.zeros((rows, D_MODEL), F32).at[:batch].set(c)
    tn = 1024
    out = pl.pallas_call(
        _mod_kernel,
        grid=(depth, n_out // tn),
        in_specs=[
            pl.BlockSpec((rows, D_MODEL), lambda l, j: (0, 0)),
            pl.BlockSpec((None, D_MODEL, tn), lambda l, j: (l, 0, j)),
            pl.BlockSpec((None, 1, tn), lambda l, j: (l, 0, j)),
        ],
        out_specs=pl.BlockSpec((None, rows, tn), lambda l, j: (l, 0, j)),
        out_shape=jax.ShapeDtypeStruct((depth, rows, n_out), F32),
        compiler_params=_params(2),
        name="adaln_modulation",
    )(c_pad, w_ada, b_ada.reshape(depth, 1, n_out))
    return out[:, :batch].reshape(depth, batch, N_MOD, 1, D_MODEL)


def _mod_spec(layer, which, rows_per_batch_tile, m_axis):
    def index_map(*ids):
        return (layer, ids[m_axis] // rows_per_batch_tile, which, 0, 0)
    return pl.BlockSpec((None, None, None, 1, D_MODEL), index_map)


def _modulate_kernel(x_ref, sc_ref, sh_ref, h_ref):
    h_ref[...] = (x_ref[...] * (1.0 + sc_ref[...]) + sh_ref[...]).astype(BF16)


def _modulate(x2, mod, layer, seq):
    m_rows = x2.shape[0]
    tm = 1024
    return pl.pallas_call(
        _modulate_kernel,
        grid=(m_rows // tm,),
        in_specs=[
            pl.BlockSpec((tm, D_MODEL), lambda m: (m, 0)),
            _mod_spec(layer, MOD_SC_A, seq // tm, 0),
            _mod_spec(layer, MOD_SH_A, seq // tm, 0),
        ],
        out_specs=pl.BlockSpec((tm, D_MODEL), lambda m: (m, 0)),
        out_shape=jax.ShapeDtypeStruct((m_rows, D_MODEL), BF16),
        compiler_params=_params(1),
        name="modulate_in",
    )(x2, mod, mod)


def _rotary(t, cos_t, sin_lo, sin_hi):
    half = ROPE_DIM // 2
    return (t * cos_t + pltpu.roll(t, HEAD_DIM - half, 1) * sin_lo + pltpu.roll(t, half, 1) * sin_hi)


LOG2_E = 1.4426950408889634
Q_PRESCALE = HEAD_DIM ** -0.5 * LOG2_E


def _qk_kernel(h_ref, w_ref, cos_ref, slo_ref, shi_ref, o_ref, wbf_ref, *, n_q_tiles, heads_per_tile, row_chunk):
    n = pl.program_id(0)

    @pl.when(pl.program_id(1) == 0)
    def _():
        wbf_ref[...] = w_ref[...].astype(BF16)

    head_scale = jnp.where(n < n_q_tiles, Q_PRESCALE, 1.0).astype(F32)
    for c in range(h_ref.shape[0] // row_chunk):
        rows = slice(c * row_chunk, (c + 1) * row_chunk)
        acc = jnp.dot(h_ref[rows, :], wbf_ref[...], preferred_element_type=F32)
        cos_t, sin_lo, sin_hi = cos_ref[rows, :], slo_ref[rows, :], shi_ref[rows, :]
        for j in range(heads_per_tile):
            sl = slice(j * HEAD_DIM, (j + 1) * HEAD_DIM)
            o_ref[rows, sl] = _rotary(acc[:, sl] * head_scale, cos_t, sin_lo, sin_hi).astype(BF16)


def _qk_proj(h, w_in, layer, rope, seq):
    m_rows = h.shape[0]
    tm, tn = 1024, 512
    kern = functools.partial(_qk_kernel, n_q_tiles=D_MODEL // tn, heads_per_tile=tn // HEAD_DIM, row_chunk=256)
    rope_spec = pl.BlockSpec((tm, HEAD_DIM), lambda n, m: (m % (seq // tm), 0))
    return pl.pallas_call(
        kern,
        grid=(COL_V // tn, m_rows // tm),
        in_specs=[
            pl.BlockSpec((tm, D_MODEL), lambda n, m: (m, 0)),
            pl.BlockSpec((None, D_MODEL, tn), lambda n, m: (layer, 0, n)),
            rope_spec, rope_spec, rope_spec,
        ],
        out_specs=pl.BlockSpec((tm, tn), lambda n, m: (m, n)),
        out_shape=jax.ShapeDtypeStruct((m_rows, COL_V), BF16),
        scratch_shapes=[pltpu.VMEM((D_MODEL, tn), BF16)],
        compiler_params=_params(2),
        name="qk_proj",
    )(h, w_in, *rope)


def _v_kernel(h_ref, w_ref, o_ref, wbf_ref):
    @pl.when(pl.program_id(0) == 0)
    def _():
        wbf_ref[...] = w_ref[...].astype(BF16)

    acc = jnp.dot(h_ref[...], wbf_ref[...], preferred_element_type=F32)
    o_ref[...] = acc.T.astype(BF16)


def _v_proj(h, w_in, layer, batch, seq):
    m_rows = h.shape[0]
    tm = 1024
    per_batch = seq // tm
    return pl.pallas_call(
        _v_kernel,
        grid=(m_rows // tm,),
        in_specs=[
            pl.BlockSpec((tm, D_MODEL), lambda m: (m, 0)),
            pl.BlockSpec((None, D_MODEL, D_KV), lambda m: (layer, 0, COL_V // D_KV)),
        ],
        out_specs=pl.BlockSpec((None, D_KV, tm), lambda m: (m // per_batch, 0, m % per_batch)),
        out_shape=jax.ShapeDtypeStruct((batch, D_KV, seq), BF16),
        scratch_shapes=[pltpu.VMEM((D_MODEL, D_KV), BF16)],
        compiler_params=_params(1),
        name="v_proj_t",
    )(h, w_in)


def _glu_kernel(h_ref, wa_ref, wb_ref, o_ref, wabf_ref, wbbf_ref):
    @pl.when(pl.program_id(1) == 0)
    def _():
        wabf_ref[...] = wa_ref[...].astype(BF16)
        wbbf_ref[...] = wb_ref[...].astype(BF16)

    h = h_ref[...]
    a = jnp.dot(h, wabf_ref[...], preferred_element_type=F32)
    b = jnp.dot(h, wbbf_ref[...], preferred_element_type=F32)
    o_ref[...] = a * jax.nn.sigmoid(b)


def _glu_proj(h, w_in, layer):
    m_rows = h.shape[0]
    tm, tn = 1024, 512
    a0, b0 = COL_GLU_A // tn, COL_GLU_B // tn
    return pl.pallas_call(
        _glu_kernel,
        grid=(D_MODEL // tn, m_rows // tm),
        in_specs=[
            pl.BlockSpec((tm, D_MODEL), lambda n, m: (m, 0)),
            pl.BlockSpec((None, D_MODEL, tn), lambda n, m: (layer, 0, a0 + n)),
            pl.BlockSpec((None, D_MODEL, tn), lambda n, m: (layer, 0, b0 + n)),
        ],
        out_specs=pl.BlockSpec((tm, tn), lambda n, m: (m, n)),
        out_shape=jax.ShapeDtypeStruct((m_rows, D_MODEL), F32),
        scratch_shapes=[pltpu.VMEM((D_MODEL, tn), BF16)] * 2,
        compiler_params=_params(2),
        name="glu_proj",
    )(h, w_in, w_in)


def _attn_kernel(sink_ref, cap_lo_ref, cap_hi_ref, q_ref, kp_ref, km_ref, kn_ref, vp_ref, vm_ref, vn_ref, o_ref,
                 *, tq, n_tiles):
    i = pl.program_id(1)
    n_blk = tq // WINDOW
    band = 3 * WINDOW
    q_cols = GQA_GROUP * HEAD_DIM
    cap_lo, cap_hi = cap_lo_ref[...], cap_hi_ref[...]
    first_cap = jnp.where(i == 0, NEG_INF, jnp.inf).astype(F32)
    last_cap = jnp.where(i == n_tiles - 1, NEG_INF, jnp.inf).astype(F32)

    def head_inputs(hkv):
        hs = slice(hkv * HEAD_DIM, (hkv + 1) * HEAD_DIM)
        k_all = jnp.concatenate([kp_ref[:, hs], km_ref[:, hs], kn_ref[:, hs]], axis=0)
        vt_all = jnp.concatenate([vp_ref[hs, :], vm_ref[hs, :], vn_ref[hs, :]], axis=1)
        sink_row = jnp.concatenate(
            [jnp.full((1, WINDOW), sink_ref[hkv * GQA_GROUP + g] * LOG2_E, F32) for g in range(GQA_GROUP)], axis=1)
        return k_all, vt_all, sink_row

    def scores(unit, k_all):
        hkv, j = unit
        qs = jnp.concatenate(
            [q_ref[j * WINDOW:(j + 1) * WINDOW, (hkv * GQA_GROUP + g) * HEAD_DIM:(hkv * GQA_GROUP + g + 1) * HEAD_DIM]
             for g in range(GQA_GROUP)], axis=0)
        kb = k_all[j * WINDOW:j * WINDOW + band]
        return lax.dot_general(kb, qs, (((1,), (1,)), ((), ())), preferred_element_type=F32)

    units = [(hkv, j) for hkv in range(N_KV_HEADS) for j in range(n_blk)]
    inputs = {hkv: head_inputs(hkv) for hkv in range(N_KV_HEADS)}
    s_next = scores(units[0], inputs[0][0])
    for idx, (hkv, j) in enumerate(units):
        _, vt_all, sink_row = inputs[hkv]
        s = s_next
        if idx + 1 < len(units):
            s_next = scores(units[idx + 1], inputs[units[idx + 1][0]][0])
        s_lo = jnp.minimum(s[:WINDOW], cap_lo)
        s_mid = s[WINDOW:2 * WINDOW]
        s_hi = jnp.minimum(s[2 * WINDOW:], cap_hi)
        if j == 0:
            s_lo = jnp.minimum(s_lo, first_cap)
        if j == n_blk - 1:
            s_hi = jnp.minimum(s_hi, last_cap)
        mx = jnp.maximum(jnp.maximum(jnp.max(s_lo, axis=0, keepdims=True), jnp.max(s_mid, axis=0, keepdims=True)),
                         jnp.maximum(jnp.max(s_hi, axis=0, keepdims=True), sink_row))
        p_lo, p_mid, p_hi = jnp.exp2(s_lo - mx), jnp.exp2(s_mid - mx), jnp.exp2(s_hi - mx)
        denom = (jnp.sum(p_lo, axis=0, keepdims=True) + jnp.sum(p_mid, axis=0, keepdims=True)
                 + jnp.sum(p_hi, axis=0, keepdims=True) + jnp.exp2(sink_row - mx))
        p_t = jnp.concatenate([p_lo, p_mid, p_hi], axis=0).astype(BF16)
        o_t = jnp.dot(vt_all[:, j * WINDOW:j * WINDOW + band], p_t, preferred_element_type=F32) * (1.0 / denom)
        for g in range(GQA_GROUP):
            col0 = hkv * q_cols + g * HEAD_DIM
            o_ref[j * WINDOW:(j + 1) * WINDOW, col0:col0 + HEAD_DIM] = (
                o_t[:, g * WINDOW:(g + 1) * WINDOW].T.astype(BF16))


def _window_caps():
    key = jnp.arange(WINDOW)[:, None]
    qry = jnp.tile(jnp.arange(WINDOW), GQA_GROUP)[None, :]
    cap_lo = jnp.where(key >= qry, jnp.inf, NEG_INF).astype(F32)
    cap_hi = jnp.where(key <= qry, jnp.inf, NEG_INF).astype(F32)
    return cap_lo, cap_hi


def _attention(qk, v_t, sink_l, batch, seq):
    tq = 512
    n_blk = seq // WINDOW
    per_tile = tq // WINDOW
    n_tiles = seq // tq
    qk3 = qk.reshape(batch, seq, COL_V)
    k_col = COL_K // D_KV
    prev_blk = lambda i: jnp.maximum(i * per_tile - 1, 0)
    next_blk = lambda i: jnp.minimum((i + 1) * per_tile, n_blk - 1)
    cap_spec = pl.BlockSpec((WINDOW, GQA_GROUP * HEAD_DIM), lambda b, i: (0, 0))
    kern = functools.partial(_attn_kernel, tq=tq, n_tiles=n_tiles)
    out = pl.pallas_call(
        kern,
        grid=(batch, n_tiles),
        in_specs=[
            pl.BlockSpec(memory_space=pltpu.SMEM),
            cap_spec, cap_spec,
            pl.BlockSpec((None, tq, D_MODEL), lambda b, i: (b, i, 0)),
            pl.BlockSpec((None, WINDOW, D_KV), lambda b, i: (b, prev_blk(i), k_col)),
            pl.BlockSpec((None, tq, D_KV), lambda b, i: (b, i, k_col)),
            pl.BlockSpec((None, WINDOW, D_KV), lambda b, i: (b, next_blk(i), k_col)),
            pl.BlockSpec((None, D_KV, WINDOW), lambda b, i: (b, 0, prev_blk(i))),
            pl.BlockSpec((None, D_KV, tq), lambda b, i: (b, 0, i)),
            pl.BlockSpec((None, D_KV, WINDOW), lambda b, i: (b, 0, next_blk(i))),
        ],
        out_specs=pl.BlockSpec((None, tq, D_MODEL), lambda b, i: (b, i, 0)),
        out_shape=jax.ShapeDtypeStruct((batch, seq, D_MODEL), BF16),
        compiler_params=_params(2),
        name="window_attention",
    )(sink_l, *_window_caps(), qk3, qk3, qk3, qk3, v_t, v_t, v_t)
    return out.reshape(batch * seq, D_MODEL)


def _conv_kernel(up_ref, um_ref, un_ref, w_ref, g_ref, b_ref, o_ref, xs_ref, acc_ref, *, ts, n_tiles):
    i = pl.program_id(1)
    xs_ref[0:CONV_HALO] = jnp.where(i > 0, up_ref[...], 0.0)
    xs_ref[CONV_HALO:CONV_HALO + ts] = um_ref[...]
    xs_ref[CONV_HALO + ts:] = jnp.where(i < n_tiles - 1, un_ref[...], 0.0)

    rb, cb = 64, 128
    first = CONV_HALO - CONV_WIDTH // 2
    win = rb + 2 * CONV_HALO

    def row_chunk(rc, carry):
        r0 = pl.multiple_of(rc * rb, rb)
        for cc in range(D_MODEL // cb):
            cs = slice(cc * cb, (cc + 1) * cb)
            window = xs_ref[pl.ds(r0, win), cs]
            acc = jnp.zeros((rb, cb), F32)
            for r in range(8):
                shifted = window if r == 0 else pltpu.roll(window, win - r, 0)
                for a in range(win // 8):
                    k = 8 * a + r - first
                    if 0 <= k < CONV_WIDTH:
                        acc = acc + shifted[8 * a:8 * a + rb] * w_ref[k:k + 1, cs]
            acc_ref[pl.ds(r0, rb), cs] = acc
        return carry

    lax.fori_loop(0, ts // rb, row_chunk, 0)
    y = _layer_norm(acc_ref[...], g_ref[...], b_ref[...])
    o_ref[...] = (y * jax.nn.sigmoid(y)).astype(BF16)


def _conformer_conv(u, w_dw, ln_g, ln_b, layer, batch, seq):
    ts = 256
    n_tiles = seq // ts
    per_tile = ts // CONV_HALO
    n_halo = seq // CONV_HALO
    u3 = u.reshape(batch, seq, D_MODEL)
    depth = w_dw.shape[0]
    kern = functools.partial(_conv_kernel, ts=ts, n_tiles=n_tiles)
    vec = pl.BlockSpec((None, 1, D_MODEL), lambda b, i: (layer, 0, 0))
    out = pl.pallas_call(
        kern,
        grid=(batch, n_tiles),
        in_specs=[
            pl.BlockSpec((None, CONV_HALO, D_MODEL), lambda b, i: (b, jnp.maximum(i * per_tile - 1, 0), 0)),
            pl.BlockSpec((None, ts, D_MODEL), lambda b, i: (b, i, 0)),
            pl.BlockSpec((None, CONV_HALO, D_MODEL),
                         lambda b, i: (b, jnp.minimum((i + 1) * per_tile, n_halo - 1), 0)),
            pl.BlockSpec((None, CONV_WIDTH, D_MODEL), lambda b, i: (layer, 0, 0)),
            vec, vec,
        ],
        out_specs=pl.BlockSpec((None, ts, D_MODEL), lambda b, i: (b, i, 0)),
        out_shape=jax.ShapeDtypeStruct((batch, seq, D_MODEL), BF16),
        scratch_shapes=[pltpu.VMEM((ts + 2 * CONV_HALO, D_MODEL), F32), pltpu.VMEM((ts, D_MODEL), F32)],
        compiler_params=_params(2),
        name="conformer_conv",
    )(u3, u3, u3, w_dw, ln_g.reshape(depth, 1, D_MODEL), ln_b.reshape(depth, 1, D_MODEL))
    return out.reshape(batch * seq, D_MODEL)


def _merge_kernel(h_ref, o_ref, cv_ref, wga_ref, wgb_ref, woa_ref, wob_ref, out_ref, wgabf_ref, wgbbf_ref):
    @pl.when(pl.program_id(1) == 0)
    def _():
        wgabf_ref[...] = wga_ref[...].astype(BF16)
        wgbbf_ref[...] = wgb_ref[...].astype(BF16)

    h = h_ref[...]
    g_a = jnp.dot(h, wgabf_ref[...], preferred_element_type=F32)
    g_b = jnp.dot(h, wgbbf_ref[...], preferred_element_type=F32)
    y_a = jnp.dot(o_ref[...], woa_ref[...], preferred_element_type=F32)
    y_b = jnp.dot(cv_ref[...], wob_ref[...], preferred_element_type=F32)
    out_ref[...] = (jax.nn.sigmoid(g_a) * y_a + jax.nn.sigmoid(g_b) * y_b).astype(BF16)


def _merge(h, o, cv, w_in, w_oa_bf, w_ob_bf, layer):
    m_rows = h.shape[0]
    tm, tn = 512, 512
    ga0, gb0 = COL_GATE_A // tn, COL_GATE_B // tn
    act = pl.BlockSpec((tm, D_MODEL), lambda n, m: (m, 0))
    wsq = pl.BlockSpec((None, D_MODEL, tn), lambda n, m: (layer, 0, n))
    return pl.pallas_call(
        _merge_kernel,
        grid=(D_MODEL // tn, m_rows // tm),
        in_specs=[
            act, act, act,
            pl.BlockSpec((None, D_MODEL, tn), lambda n, m: (layer, 0, ga0 + n)),
            pl.BlockSpec((None, D_MODEL, tn), lambda n, m: (layer, 0, gb0 + n)),
            wsq, wsq,
        ],
        out_specs=pl.BlockSpec((tm, tn), lambda n, m: (m, n)),
        out_shape=jax.ShapeDtypeStruct((m_rows, D_MODEL), BF16),
        scratch_shapes=[pltpu.VMEM((D_MODEL, tn), BF16)] * 2,
        compiler_params=_params(2),
        name="branch_merge",
    )(h, o, cv, w_in, w_in, w_oa_bf, w_ob_bf)


def _proj_norm_kernel(a_ref, w_ref, x_ref, gt_ref, g_ref, b_ref, *rest, emit_h, row_chunk):
    if emit_h:
        sc_ref, sh_ref, xo_ref, ho_ref = rest
    else:
        (xo_ref,) = rest
    for c in range(a_ref.shape[0] // row_chunk):
        rows = slice(c * row_chunk, (c + 1) * row_chunk)
        y = jnp.dot(a_ref[rows, :], w_ref[...], preferred_element_type=F32)
        z = DEEPNORM_ALPHA * x_ref[rows, :] + (1.0 + gt_ref[...]) * y
        xn = _layer_norm(z, g_ref[...], b_ref[...])
        xo_ref[rows, :] = xn
        if emit_h:
            ho_ref[rows, :] = (xn * (1.0 + sc_ref[...]) + sh_ref[...]).astype(BF16)


def _proj_norm(a, w_bf, x2, mod, ln_g, ln_b, layer, gate_idx, next_mod, seq, tm, row_chunk, name):
    m_rows, k_dim = a.shape
    depth = ln_g.shape[0]
    per_batch = seq // tm
    vec = pl.BlockSpec((None, 1, D_MODEL), lambda m: (layer, 0, 0))
    in_specs = [
        pl.BlockSpec((tm, k_dim), lambda m: (m, 0)),
        pl.BlockSpec((None, k_dim, D_MODEL), lambda m: (layer, 0, 0), pipeline_mode=pl.Buffered(1)),
        pl.BlockSpec((tm, D_MODEL), lambda m: (m, 0)),
        _mod_spec(layer, gate_idx, per_batch, 0),
        vec, vec,
    ]
    args = [a, w_bf, x2, mod, ln_g.reshape(depth, 1, D_MODEL), ln_b.reshape(depth, 1, D_MODEL)]
    row_out = pl.BlockSpec((tm, D_MODEL), lambda m: (m, 0))
    emit_h = next_mod is not None
    if emit_h:
        nl, sc_idx, sh_idx = next_mod
        in_specs += [_mod_spec(nl, sc_idx, per_batch, 0), _mod_spec(nl, sh_idx, per_batch, 0)]
        args += [mod, mod]
        out_specs = [row_out, row_out]
        out_shape = [jax.ShapeDtypeStruct((m_rows, D_MODEL), F32), jax.ShapeDtypeStruct((m_rows, D_MODEL), BF16)]
    else:
        out_specs = [row_out]
        out_shape = [jax.ShapeDtypeStruct((m_rows, D_MODEL), F32)]
    outs = pl.pallas_call(
        functools.partial(_proj_norm_kernel, emit_h=emit_h, row_chunk=row_chunk),
        grid=(m_rows // tm,),
        in_specs=in_specs,
        out_specs=out_specs,
        out_shape=out_shape,
        compiler_params=_params(1),
        name=name,
    )(*args)
    return (outs[0], outs[1]) if emit_h else (outs[0], None)


def _swiglu_kernel(h_ref, wg_ref, wu_ref, o_ref, wgbf_ref, wubf_ref):
    @pl.when(pl.program_id(1) == 0)
    def _():
        wgbf_ref[...] = wg_ref[...].astype(BF16)
        wubf_ref[...] = wu_ref[...].astype(BF16)

    h = h_ref[...]
    gate = jnp.dot(h, wgbf_ref[...], preferred_element_type=F32)
    up = jnp.dot(h, wubf_ref[...], preferred_element_type=F32)
    o_ref[...] = (gate * jax.nn.sigmoid(gate) * up).astype(BF16)


def _swiglu_up(h, w_gu, layer):
    m_rows = h.shape[0]
    tm, tn = 1024, 512
    up0 = D_FF // tn
    return pl.pallas_call(
        _swiglu_kernel,
        grid=(D_FF // tn, m_rows // tm),
        in_specs=[
            pl.BlockSpec((tm, D_MODEL), lambda n, m: (m, 0)),
            pl.BlockSpec((None, D_MODEL, tn), lambda n, m: (layer, 0, n)),
            pl.BlockSpec((None, D_MODEL, tn), lambda n, m: (layer, 0, up0 + n)),
        ],
        out_specs=pl.BlockSpec((tm, tn), lambda n, m: (m, n)),
        out_shape=jax.ShapeDtypeStruct((m_rows, D_FF), BF16),
        scratch_shapes=[pltpu.VMEM((D_MODEL, tn), BF16)] * 2,
        compiler_params=_params(2),
        name="swiglu_up",
    )(h, w_gu, w_gu)


def _rope_tables(seq):
    half = ROPE_DIM // 2
    pos = jnp.arange(seq, dtype=F32)
    inv_freq = ROPE_THETA ** (-jnp.arange(0, ROPE_DIM, 2, dtype=F32) / ROPE_DIM)
    ang = pos[:, None] * inv_freq[None, :]
    cos, sin = jnp.cos(ang), jnp.sin(ang)
    zeros = lambda n: jnp.zeros((seq, n), F32)
    cos_t = jnp.concatenate([cos, cos, jnp.ones((seq, HEAD_DIM - ROPE_DIM), F32)], axis=1)
    sin_lo = jnp.concatenate([-sin, zeros(HEAD_DIM - half)], axis=1)
    sin_hi = jnp.concatenate([zeros(half), sin, zeros(HEAD_DIM - ROPE_DIM)], axis=1)
    return cos_t, sin_lo, sin_hi


def kernel(x, c, w_ada, b_ada, w_in, sink, w_dw, conv_ln_g, conv_ln_b, w_oa, w_ob, w_out,
           ln1_g, ln1_b, w_gu, w_down, ln2_g, ln2_b):
    batch, seq, _ = x.shape
    depth = w_in.shape[0]
    mod = _modulation(c, w_ada, b_ada)
    rope = _rope_tables(seq)
    w_oa_bf, w_ob_bf = w_oa.astype(BF16), w_ob.astype(BF16)
    w_out_bf, w_down_bf = w_out.astype(BF16), w_down.astype(BF16)

    x2 = x.reshape(batch * seq, D_MODEL)
    h = _modulate(x2, mod, 0, seq)
    for l in range(depth):
        qk = _qk_proj(h, w_in, l, rope, seq)
        v_t = _v_proj(h, w_in, l, batch, seq)
        u = _glu_proj(h, w_in, l)
        attn = _attention(qk, v_t, sink[l], batch, seq)
        cv = _conformer_conv(u, w_dw, conv_ln_g, conv_ln_b, l, batch, seq)
        merged = _merge(h, attn, cv, w_in, w_oa_bf, w_ob_bf, l)
        x2, h = _proj_norm(merged, w_out_bf, x2, mod, ln1_g, ln1_b, l, MOD_GT_A,
                           (l, MOD_SC_F, MOD_SH_F), seq, 512, 256, "attn_out_norm")
        act = _swiglu_up(h, w_gu, l)
        next_mod = (l + 1, MOD_SC_A, MOD_SH_A) if l + 1 < depth else None
        x2, h = _proj_norm(act, w_down_bf, x2, mod, ln2_g, ln2_b, l, MOD_GT_F,
                           next_mod, seq, 256, 128, "ffn_down_norm")
    return x2.reshape(batch, seq, D_MODEL)
```

```python
import functools

import jax
import jax.numpy as jnp
from jax import lax
from jax.experimental import pallas as pl
from jax.experimental.pallas import tpu as pltpu

F32 = jnp.float32
BF16 = jnp.bfloat16

D_MODEL = 2048
HEAD_DIM = 128
N_Q_HEADS = 16
N_KV_HEADS = 4
GQA_GROUP = 4
WINDOW = 128
ROPE_DIM = 32
ROPE_THETA = 500000.0
CONV_WIDTH = 31
CONV_HALO = 16
D_FF = 5632
D_KV = N_KV_HEADS * HEAD_DIM
D_IN = D_MODEL + 2 * D_KV + 4 * D_MODEL
N_MOD = 6
DEPTH = 4
DEEPNORM_ALPHA = (2.0 * DEPTH) ** 0.25
LN_EPS = 1e-5
NEG_INF = -1e30

COL_Q, COL_K, COL_V = 0, D_MODEL, D_MODEL + D_KV
COL_GLU_A = D_MODEL + 2 * D_KV
COL_GLU_B = COL_GLU_A + D_MODEL
COL_GATE_A = COL_GLU_B + D_MODEL
COL_GATE_B = COL_GATE_A + D_MODEL

VMEM_LIMIT_BYTES = 56 * 1024 * 1024

ROW_CHUNK = 512

MOD_SH_A, MOD_SC_A, MOD_GT_A, MOD_SH_F, MOD_SC_F, MOD_GT_F = range(N_MOD)


def _params(n_axes):
    return pltpu.CompilerParams(dimension_semantics=("arbitrary",) * n_axes,
                                vmem_limit_bytes=VMEM_LIMIT_BYTES)


def _layer_norm(z, g, b):
    mu = jnp.mean(z, axis=-1, keepdims=True)
    zc = z - mu
    var = jnp.mean(zc * zc, axis=-1, keepdims=True)
    return zc * lax.rsqrt(var + LN_EPS) * g + b


def _mod_kernel(c_ref, w_ref, b_ref, o_ref):
    c = c_ref[...]
    c_act = (c * jax.nn.sigmoid(c)).astype(BF16)
    o_ref[...] = jnp.dot(c_act, w_ref[...].astype(BF16), preferred_element_type=F32) + b_ref[...]


def _modulation(c, w_ada, b_ada):
    depth, _, n_out = w_ada.shape
    batch = c.shape[0]
    rows = 8
    c_pad = jnp.zeros((rows, D_MODEL), F32).at[:batch].set(c)
    tn = 1024
    out = pl.pallas_call(
        _mod_kernel,
        grid=(depth, n_out // tn),
        in_specs=[
            pl.BlockSpec((rows, D_MODEL), lambda l, j: (0, 0)),
            pl.BlockSpec((None, D_MODEL, tn), lambda l, j: (l, 0, j)),
            pl.BlockSpec((None, 1, tn), lambda l, j: (l, 0, j)),
        ],
        out_specs=pl.BlockSpec((None, rows, tn), lambda l, j: (l, 0, j)),
        out_shape=jax.ShapeDtypeStruct((depth, rows, n_out), F32),
        compiler_params=_params(2),
        name="adaln_modulation",
    )(c_pad, w_ada, b_ada.reshape(depth, 1, n_out))
    return out[:, :batch].reshape(depth, batch, N_MOD, 1, D_MODEL)


def _mod_spec(layer, which, rows_per_batch_tile, m_axis):
    def index_map(*ids):
        return (layer, ids[m_axis] // rows_per_batch_tile, which, 0, 0)
    return pl.BlockSpec((None, None, None, 1, D_MODEL), index_map)


def _modulate_kernel(x_ref, sc_ref, sh_ref, h_ref):
    h_ref[...] = (x_ref[...] * (1.0 + sc_ref[...]) + sh_ref[...]).astype(BF16)


def _modulate(x2, mod, layer, seq):
    m_rows = x2.shape[0]
    tm = 1024
    return pl.pallas_call(
        _modulate_kernel,
        grid=(m_rows // tm,),
        in_specs=[
            pl.BlockSpec((tm, D_MODEL), lambda m: (m, 0)),
            _mod_spec(layer, MOD_SC_A, seq // tm, 0),
            _mod_spec(layer, MOD_SH_A, seq // tm, 0),
        ],
        out_specs=pl.BlockSpec((tm, D_MODEL), lambda m: (m, 0)),
        out_shape=jax.ShapeDtypeStruct((m_rows, D_MODEL), BF16),
        compiler_params=_params(1),
        name="modulate_in",
    )(x2, mod, mod)


def _rotary(t, cos_t, sin_lo, sin_hi):
    half = ROPE_DIM // 2
    return (t * cos_t + pltpu.roll(t, HEAD_DIM - half, 1) * sin_lo + pltpu.roll(t, half, 1) * sin_hi)


LOG2_E = 1.4426950408889634
Q_PRESCALE = HEAD_DIM ** -0.5 * LOG2_E


def _qk_kernel(h_ref, w_ref, cos_ref, slo_ref, shi_ref, o_ref, wbf_ref, *, n_q_tiles, heads_per_tile, row_chunk):
    n = pl.program_id(0)

    @pl.when(pl.program_id(1) == 0)
    def _():
        wbf_ref[...] = w_ref[...].astype(BF16)

    head_scale = jnp.where(n < n_q_tiles, Q_PRESCALE, 1.0).astype(F32)
    for c in range(h_ref.shape[0] // row_chunk):
        rows = slice(c * row_chunk, (c + 1) * row_chunk)
        acc = jnp.dot(h_ref[rows, :], wbf_ref[...], preferred_element_type=F32)
        cos_t, sin_lo, sin_hi = cos_ref[rows, :], slo_ref[rows, :], shi_ref[rows, :]
        for j in range(heads_per_tile):
            sl = slice(j * HEAD_DIM, (j + 1) * HEAD_DIM)
            o_ref[rows, sl] = _rotary(acc[:, sl] * head_scale, cos_t, sin_lo, sin_hi).astype(BF16)


def _qk_proj(h, w_in, layer, rope, seq):
    m_rows = h.shape[0]
    tm, tn = 2048, 512
    kern = functools.partial(_qk_kernel, n_q_tiles=D_MODEL // tn, heads_per_tile=tn // HEAD_DIM, row_chunk=256)
    rope_spec = pl.BlockSpec((tm, HEAD_DIM), lambda n, m: (m % (seq // tm), 0))
    return pl.pallas_call(
        kern,
        grid=(COL_V // tn, m_rows // tm),
        in_specs=[
            pl.BlockSpec((tm, D_MODEL), lambda n, m: (m, 0)),
            pl.BlockSpec((None, D_MODEL, tn), lambda n, m: (layer, 0, n)),
            rope_spec, rope_spec, rope_spec,
        ],
        out_specs=pl.BlockSpec((tm, tn), lambda n, m: (m, n)),
        out_shape=jax.ShapeDtypeStruct((m_rows, COL_V), BF16),
        scratch_shapes=[pltpu.VMEM((D_MODEL, tn), BF16)],
        compiler_params=_params(2),
        name="qk_proj",
    )(h, w_in, *rope)


def _v_kernel(h_ref, w_ref, o_ref, wbf_ref):
    @pl.when(pl.program_id(0) == 0)
    def _():
        wbf_ref[...] = w_ref[...].astype(BF16)

    acc = jnp.dot(h_ref[...], wbf_ref[...], preferred_element_type=F32)
    o_ref[...] = acc.T.astype(BF16)


def _v_proj(h, w_in, layer, batch, seq):
    m_rows = h.shape[0]
    tm = 1024
    per_batch = seq // tm
    return pl.pallas_call(
        _v_kernel,
        grid=(m_rows // tm,),
        in_specs=[
            pl.BlockSpec((tm, D_MODEL), lambda m: (m, 0)),
            pl.BlockSpec((None, D_MODEL, D_KV), lambda m: (layer, 0, COL_V // D_KV)),
        ],
        out_specs=pl.BlockSpec((None, D_KV, tm), lambda m: (m // per_batch, 0, m % per_batch)),
        out_shape=jax.ShapeDtypeStruct((batch, D_KV, seq), BF16),
        scratch_shapes=[pltpu.VMEM((D_MODEL, D_KV), BF16)],
        compiler_params=_params(1),
        name="v_proj_t",
    )(h, w_in)


def _glu_kernel(h_ref, wa_ref, wb_ref, o_ref, wabf_ref, wbbf_ref):
    @pl.when(pl.program_id(1) == 0)
    def _():
        wabf_ref[...] = wa_ref[...].astype(BF16)
        wbbf_ref[...] = wb_ref[...].astype(BF16)

    for c in range(h_ref.shape[0] // ROW_CHUNK):
        rows = slice(c * ROW_CHUNK, (c + 1) * ROW_CHUNK)
        a = jnp.dot(h_ref[rows, :], wabf_ref[...], preferred_element_type=F32)
        b = jnp.dot(h_ref[rows, :], wbbf_ref[...], preferred_element_type=F32)
        o_ref[rows, :] = a * jax.nn.sigmoid(b)


def _glu_proj(h, w_in, layer):
    m_rows = h.shape[0]
    tm, tn = 2048, 512
    a0, b0 = COL_GLU_A // tn, COL_GLU_B // tn
    return pl.pallas_call(
        _glu_kernel,
        grid=(D_MODEL // tn, m_rows // tm),
        in_specs=[
            pl.BlockSpec((tm, D_MODEL), lambda n, m: (m, 0)),
            pl.BlockSpec((None, D_MODEL, tn), lambda n, m: (layer, 0, a0 + n)),
            pl.BlockSpec((None, D_MODEL, tn), lambda n, m: (layer, 0, b0 + n)),
        ],
        out_specs=pl.BlockSpec((tm, tn), lambda n, m: (m, n)),
        out_shape=jax.ShapeDtypeStruct((m_rows, D_MODEL), F32),
        scratch_shapes=[pltpu.VMEM((D_MODEL, tn), BF16)] * 2,
        compiler_params=_params(2),
        name="glu_proj",
    )(h, w_in, w_in)


def _attn_kernel(sink_ref, cap_lo_ref, cap_hi_ref, q_ref, kp_ref, km_ref, kn_ref, vp_ref, vm_ref, vn_ref, o_ref,
                 *, tq, n_tiles):
    i = pl.program_id(1)
    n_blk = tq // WINDOW
    band = 3 * WINDOW
    q_cols = GQA_GROUP * HEAD_DIM
    cap_lo, cap_hi = cap_lo_ref[...], cap_hi_ref[...]
    first_cap = jnp.where(i == 0, NEG_INF, jnp.inf).astype(F32)
    last_cap = jnp.where(i == n_tiles - 1, NEG_INF, jnp.inf).astype(F32)

    def head_inputs(hkv):
        hs = slice(hkv * HEAD_DIM, (hkv + 1) * HEAD_DIM)
        k_all = jnp.concatenate([kp_ref[:, hs], km_ref[:, hs], kn_ref[:, hs]], axis=0)
        vt_all = jnp.concatenate([vp_ref[hs, :], vm_ref[hs, :], vn_ref[hs, :]], axis=1)
        sink_row = jnp.concatenate(
            [jnp.full((1, WINDOW), sink_ref[hkv * GQA_GROUP + g] * LOG2_E, F32) for g in range(GQA_GROUP)], axis=1)
        return k_all, vt_all, sink_row

    def scores(unit, k_all):
        hkv, j = unit
        qs = jnp.concatenate(
            [q_ref[j * WINDOW:(j + 1) * WINDOW, (hkv * GQA_GROUP + g) * HEAD_DIM:(hkv * GQA_GROUP + g + 1) * HEAD_DIM]
             for g in range(GQA_GROUP)], axis=0)
        kb = k_all[j * WINDOW:j * WINDOW + band]
        return lax.dot_general(kb, qs, (((1,), (1,)), ((), ())), preferred_element_type=F32)

    units = [(hkv, j) for hkv in range(N_KV_HEADS) for j in range(n_blk)]
    inputs = {hkv: head_inputs(hkv) for hkv in range(N_KV_HEADS)}
    s_next = scores(units[0], inputs[0][0])
    for idx, (hkv, j) in enumerate(units):
        _, vt_all, sink_row = inputs[hkv]
        s = s_next
        if idx + 1 < len(units):
            s_next = scores(units[idx + 1], inputs[units[idx + 1][0]][0])
        s_lo = jnp.minimum(s[:WINDOW], cap_lo)
        s_mid = s[WINDOW:2 * WINDOW]
        s_hi = jnp.minimum(s[2 * WINDOW:], cap_hi)
        if j == 0:
            s_lo = jnp.minimum(s_lo, first_cap)
        if j == n_blk - 1:
            s_hi = jnp.minimum(s_hi, last_cap)
        mx = jnp.maximum(jnp.maximum(jnp.max(s_lo, axis=0, keepdims=True), jnp.max(s_mid, axis=0, keepdims=True)),
                         jnp.maximum(jnp.max(s_hi, axis=0, keepdims=True), sink_row))
        p_lo, p_mid, p_hi = jnp.exp2(s_lo - mx), jnp.exp2(s_mid - mx), jnp.exp2(s_hi - mx)
        denom = (jnp.sum(p_lo, axis=0, keepdims=True) + jnp.sum(p_mid, axis=0, keepdims=True)
                 + jnp.sum(p_hi, axis=0, keepdims=True) + jnp.exp2(sink_row - mx))
        p_t = jnp.concatenate([p_lo, p_mid, p_hi], axis=0).astype(BF16)
        o_t = jnp.dot(vt_all[:, j * WINDOW:j * WINDOW + band], p_t, preferred_element_type=F32) * (1.0 / denom)
        for g in range(GQA_GROUP):
            col0 = hkv * q_cols + g * HEAD_DIM
            o_ref[j * WINDOW:(j + 1) * WINDOW, col0:col0 + HEAD_DIM] = (
                o_t[:, g * WINDOW:(g + 1) * WINDOW].T.astype(BF16))


def _window_caps():
    key = jnp.arange(WINDOW)[:, None]
    qry = jnp.tile(jnp.arange(WINDOW), GQA_GROUP)[None, :]
    cap_lo = jnp.where(key >= qry, jnp.inf, NEG_INF).astype(F32)
    cap_hi = jnp.where(key <= qry, jnp.inf, NEG_INF).astype(F32)
    return cap_lo, cap_hi


def _attention(qk, v_t, sink_l, batch, seq):
    tq = 512
    n_blk = seq // WINDOW
    per_tile = tq // WINDOW
    n_tiles = seq // tq
    qk3 = qk.reshape(batch, seq, COL_V)
    k_col = COL_K // D_KV
    prev_blk = lambda i: jnp.maximum(i * per_tile - 1, 0)
    next_blk = lambda i: jnp.minimum((i + 1) * per_tile, n_blk - 1)
    cap_spec = pl.BlockSpec((WINDOW, GQA_GROUP * HEAD_DIM), lambda b, i: (0, 0))
    kern = functools.partial(_attn_kernel, tq=tq, n_tiles=n_tiles)
    out = pl.pallas_call(
        kern,
        grid=(batch, n_tiles),
        in_specs=[
            pl.BlockSpec(memory_space=pltpu.SMEM),
            cap_spec, cap_spec,
            pl.BlockSpec((None, tq, D_MODEL), lambda b, i: (b, i, 0)),
            pl.BlockSpec((None, WINDOW, D_KV), lambda b, i: (b, prev_blk(i), k_col)),
            pl.BlockSpec((None, tq, D_KV), lambda b, i: (b, i, k_col)),
            pl.BlockSpec((None, WINDOW, D_KV), lambda b, i: (b, next_blk(i), k_col)),
            pl.BlockSpec((None, D_KV, WINDOW), lambda b, i: (b, 0, prev_blk(i))),
            pl.BlockSpec((None, D_KV, tq), lambda b, i: (b, 0, i)),
            pl.BlockSpec((None, D_KV, WINDOW), lambda b, i: (b, 0, next_blk(i))),
        ],
        out_specs=pl.BlockSpec((None, tq, D_MODEL), lambda b, i: (b, i, 0)),
        out_shape=jax.ShapeDtypeStruct((batch, seq, D_MODEL), BF16),
        compiler_params=_params(2),
        name="window_attention",
    )(sink_l, *_window_caps(), qk3, qk3, qk3, qk3, v_t, v_t, v_t)
    return out.reshape(batch * seq, D_MODEL)


def _conv_kernel(up_ref, um_ref, un_ref, w_ref, g_ref, b_ref, o_ref, xs_ref, acc_ref, *, ts, n_tiles):
    i = pl.program_id(1)
    xs_ref[0:CONV_HALO] = jnp.where(i > 0, up_ref[...], 0.0)
    xs_ref[CONV_HALO:CONV_HALO + ts] = um_ref[...]
    xs_ref[CONV_HALO + ts:] = jnp.where(i < n_tiles - 1, un_ref[...], 0.0)

    rb, cb = 64, 128
    first = CONV_HALO - CONV_WIDTH // 2
    win = rb + 2 * CONV_HALO

    def row_chunk(rc, carry):
        r0 = pl.multiple_of(rc * rb, rb)
        for cc in range(D_MODEL // cb):
            cs = slice(cc * cb, (cc + 1) * cb)
            window = xs_ref[pl.ds(r0, win), cs]
            acc = jnp.zeros((rb, cb), F32)
            for r in range(8):
                shifted = window if r == 0 else pltpu.roll(window, win - r, 0)
                for a in range(win // 8):
                    k = 8 * a + r - first
                    if 0 <= k < CONV_WIDTH:
                        acc = acc + shifted[8 * a:8 * a + rb] * w_ref[k:k + 1, cs]
            acc_ref[pl.ds(r0, rb), cs] = acc
        return carry

    lax.fori_loop(0, ts // rb, row_chunk, 0)
    y = _layer_norm(acc_ref[...], g_ref[...], b_ref[...])
    o_ref[...] = (y * jax.nn.sigmoid(y)).astype(BF16)


def _conformer_conv(u, w_dw, ln_g, ln_b, layer, batch, seq):
    ts = 256
    n_tiles = seq // ts
    per_tile = ts // CONV_HALO
    n_halo = seq // CONV_HALO
    u3 = u.reshape(batch, seq, D_MODEL)
    depth = w_dw.shape[0]
    kern = functools.partial(_conv_kernel, ts=ts, n_tiles=n_tiles)
    vec = pl.BlockSpec((None, 1, D_MODEL), lambda b, i: (layer, 0, 0))
    out = pl.pallas_call(
        kern,
        grid=(batch, n_tiles),
        in_specs=[
            pl.BlockSpec((None, CONV_HALO, D_MODEL), lambda b, i: (b, jnp.maximum(i * per_tile - 1, 0), 0)),
            pl.BlockSpec((None, ts, D_MODEL), lambda b, i: (b, i, 0)),
            pl.BlockSpec((None, CONV_HALO, D_MODEL),
                         lambda b, i: (b, jnp.minimum((i + 1) * per_tile, n_halo - 1), 0)),
            pl.BlockSpec((None, CONV_WIDTH, D_MODEL), lambda b, i: (layer, 0, 0)),
            vec, vec,
        ],
        out_specs=pl.BlockSpec((None, ts, D_MODEL), lambda b, i: (b, i, 0)),
        out_shape=jax.ShapeDtypeStruct((batch, seq, D_MODEL), BF16),
        scratch_shapes=[pltpu.VMEM((ts + 2 * CONV_HALO, D_MODEL), F32), pltpu.VMEM((ts, D_MODEL), F32)],
        compiler_params=_params(2),
        name="conformer_conv",
    )(u3, u3, u3, w_dw, ln_g.reshape(depth, 1, D_MODEL), ln_b.reshape(depth, 1, D_MODEL))
    return out.reshape(batch * seq, D_MODEL)


def _merge_kernel(h_ref, o_ref, cv_ref, wga_ref, wgb_ref, woa_ref, wob_ref, out_ref, wgabf_ref, wgbbf_ref):
    @pl.when(pl.program_id(1) == 0)
    def _():
        wgabf_ref[...] = wga_ref[...].astype(BF16)
        wgbbf_ref[...] = wgb_ref[...].astype(BF16)

    h = h_ref[...]
    g_a = jnp.dot(h, wgabf_ref[...], preferred_element_type=F32)
    g_b = jnp.dot(h, wgbbf_ref[...], preferred_element_type=F32)
    y_a = jnp.dot(o_ref[...], woa_ref[...], preferred_element_type=F32)
    y_b = jnp.dot(cv_ref[...], wob_ref[...], preferred_element_type=F32)
    out_ref[...] = (jax.nn.sigmoid(g_a) * y_a + jax.nn.sigmoid(g_b) * y_b).astype(BF16)


def _merge(h, o, cv, w_in, w_oa_bf, w_ob_bf, layer):
    m_rows = h.shape[0]
    tm, tn = 512, 512
    ga0, gb0 = COL_GATE_A // tn, COL_GATE_B // tn
    act = pl.BlockSpec((tm, D_MODEL), lambda n, m: (m, 0))
    wsq = pl.BlockSpec((None, D_MODEL, tn), lambda n, m: (layer, 0, n))
    return pl.pallas_call(
        _merge_kernel,
        grid=(D_MODEL // tn, m_rows // tm),
        in_specs=[
            act, act, act,
            pl.BlockSpec((None, D_MODEL, tn), lambda n, m: (layer, 0, ga0 + n)),
            pl.BlockSpec((None, D_MODEL, tn), lambda n, m: (layer, 0, gb0 + n)),
            wsq, wsq,
        ],
        out_specs=pl.BlockSpec((tm, tn), lambda n, m: (m, n)),
        out_shape=jax.ShapeDtypeStruct((m_rows, D_MODEL), BF16),
        scratch_shapes=[pltpu.VMEM((D_MODEL, tn), BF16)] * 2,
        compiler_params=_params(2),
        name="branch_merge",
    )(h, o, cv, w_in, w_in, w_oa_bf, w_ob_bf)


def _proj_norm_kernel(a_ref, w_ref, x_ref, gt_ref, g_ref, b_ref, *rest, emit_h, row_chunk):
    if emit_h:
        sc_ref, sh_ref, xo_ref, ho_ref = rest
    else:
        (xo_ref,) = rest
    for c in range(a_ref.shape[0] // row_chunk):
        rows = slice(c * row_chunk, (c + 1) * row_chunk)
        y = jnp.dot(a_ref[rows, :], w_ref[...], preferred_element_type=F32)
        z = DEEPNORM_ALPHA * x_ref[rows, :] + (1.0 + gt_ref[...]) * y
        xn = _layer_norm(z, g_ref[...], b_ref[...])
        xo_ref[rows, :] = xn
        if emit_h:
            ho_ref[rows, :] = (xn * (1.0 + sc_ref[...]) + sh_ref[...]).astype(BF16)


def _proj_norm(a, w_bf, x2, mod, ln_g, ln_b, layer, gate_idx, next_mod, seq, tm, row_chunk, name):
    m_rows, k_dim = a.shape
    depth = ln_g.shape[0]
    per_batch = seq // tm
    vec = pl.BlockSpec((None, 1, D_MODEL), lambda m: (layer, 0, 0))
    in_specs = [
        pl.BlockSpec((tm, k_dim), lambda m: (m, 0)),
        pl.BlockSpec((None, k_dim, D_MODEL), lambda m: (layer, 0, 0), pipeline_mode=pl.Buffered(1)),
        pl.BlockSpec((tm, D_MODEL), lambda m: (m, 0)),
        _mod_spec(layer, gate_idx, per_batch, 0),
        vec, vec,
    ]
    args = [a, w_bf, x2, mod, ln_g.reshape(depth, 1, D_MODEL), ln_b.reshape(depth, 1, D_MODEL)]
    row_out = pl.BlockSpec((tm, D_MODEL), lambda m: (m, 0))
    emit_h = next_mod is not None
    if emit_h:
        nl, sc_idx, sh_idx = next_mod
        in_specs += [_mod_spec(nl, sc_idx, per_batch, 0), _mod_spec(nl, sh_idx, per_batch, 0)]
        args += [mod, mod]
        out_specs = [row_out, row_out]
        out_shape = [jax.ShapeDtypeStruct((m_rows, D_MODEL), F32), jax.ShapeDtypeStruct((m_rows, D_MODEL), BF16)]
    else:
        out_specs = [row_out]
        out_shape = [jax.ShapeDtypeStruct((m_rows, D_MODEL), F32)]
    outs = pl.pallas_call(
        functools.partial(_proj_norm_kernel, emit_h=emit_h, row_chunk=row_chunk),
        grid=(m_rows // tm,),
        in_specs=in_specs,
        out_specs=out_specs,
        out_shape=out_shape,
        compiler_params=_params(1),
        name=name,
    )(*args)
    return (outs[0], outs[1]) if emit_h else (outs[0], None)


def _swiglu_kernel(h_ref, wg_ref, wu_ref, o_ref, wgbf_ref, wubf_ref):
    @pl.when(pl.program_id(1) == 0)
    def _():
        wgbf_ref[...] = wg_ref[...].astype(BF16)
        wubf_ref[...] = wu_ref[...].astype(BF16)

    for c in range(h_ref.shape[0] // ROW_CHUNK):
        rows = slice(c * ROW_CHUNK, (c + 1) * ROW_CHUNK)
        gate = jnp.dot(h_ref[rows, :], wgbf_ref[...], preferred_element_type=F32)
        up = jnp.dot(h_ref[rows, :], wubf_ref[...], preferred_element_type=F32)
        o_ref[rows, :] = (gate * jax.nn.sigmoid(gate) * up).astype(BF16)


def _swiglu_up(h, w_gu, layer):
    m_rows = h.shape[0]
    tm, tn = 2048, 512
    up0 = D_FF // tn
    return pl.pallas_call(
        _swiglu_kernel,
        grid=(D_FF // tn, m_rows // tm),
        in_specs=[
            pl.BlockSpec((tm, D_MODEL), lambda n, m: (m, 0)),
            pl.BlockSpec((None, D_MODEL, tn), lambda n, m: (layer, 0, n)),
            pl.BlockSpec((None, D_MODEL, tn), lambda n, m: (layer, 0, up0 + n)),
        ],
        out_specs=pl.BlockSpec((tm, tn), lambda n, m: (m, n)),
        out_shape=jax.ShapeDtypeStruct((m_rows, D_FF), BF16),
        scratch_shapes=[pltpu.VMEM((D_MODEL, tn), BF16)] * 2,
        compiler_params=_params(2),
        name="swiglu_up",
    )(h, w_gu, w_gu)


def _rope_tables(seq):
    half = ROPE_DIM // 2
    pos = jnp.arange(seq, dtype=F32)
    inv_freq = ROPE_THETA ** (-jnp.arange(0, ROPE_DIM, 2, dtype=F32) / ROPE_DIM)
    ang = pos[:, None] * inv_freq[None, :]
    cos, sin = jnp.cos(ang), jnp.sin(ang)
    zeros = lambda n: jnp.zeros((seq, n), F32)
    cos_t = jnp.concatenate([cos, cos, jnp.ones((seq, HEAD_DIM - ROPE_DIM), F32)], axis=1)
    sin_lo = jnp.concatenate([-sin, zeros(HEAD_DIM - half)], axis=1)
    sin_hi = jnp.concatenate([zeros(half), sin, zeros(HEAD_DIM - ROPE_DIM)], axis=1)
    return cos_t, sin_lo, sin_hi


def kernel(x, c, w_ada, b_ada, w_in, sink, w_dw, conv_ln_g, conv_ln_b, w_oa, w_ob, w_out,
           ln1_g, ln1_b, w_gu, w_down, ln2_g, ln2_b):
    batch, seq, _ = x.shape
    depth = w_in.shape[0]
    mod = _modulation(c, w_ada, b_ada)
    rope = _rope_tables(seq)
    w_oa_bf, w_ob_bf = w_oa.astype(BF16), w_ob.astype(BF16)
    w_out_bf, w_down_bf = w_out.astype(BF16), w_down.astype(BF16)

    x2 = x.reshape(batch * seq, D_MODEL)
    h = _modulate(x2, mod, 0, seq)
    for l in range(depth):
        qk = _qk_proj(h, w_in, l, rope, seq)
        v_t = _v_proj(h, w_in, l, batch, seq)
        u = _glu_proj(h, w_in, l)
        attn = _attention(qk, v_t, sink[l], batch, seq)
        cv = _conformer_conv(u, w_dw, conv_ln_g, conv_ln_b, l, batch, seq)
        merged = _merge(h, attn, cv, w_in, w_oa_bf, w_ob_bf, l)
        x2, h = _proj_norm(merged, w_out_bf, x2, mod, ln1_g, ln1_b, l, MOD_GT_A,
                           (l, MOD_SC_F, MOD_SH_F), seq, 512, 256, "attn_out_norm")
        act = _swiglu_up(h, w_gu, l)
        next_mod = (l + 1, MOD_SC_A, MOD_SH_A) if l + 1 < depth else None
        x2, h = _proj_norm(act, w_down_bf, x2, mod, ln2_g, ln2_b, l, MOD_GT_F,
                           next_mod, seq, 256, 128, "ffn_down_norm")
    return x2.reshape(batch, seq, D_MODEL)
```
